```python
import math
import jax, jax.numpy as jnp
from jax import lax
import numpy as np

D_MODEL = 1024
BATCH = 8
SEQ = 8192
DEPTH = 2

MIX_WIDTH = D_MODEL
CONV_CH = MIX_WIDTH // 2
CONV_WIDTH = 31
SG_CH = MIX_WIDTH - CONV_CH
SG_HEADS = 4
SG_HEAD_DIM = SG_CH // SG_HEADS
SG_CHUNK = 128
IN_AB = 2 * CONV_CH + 2 * SG_CH
DIFF_HEAD_DIM = 64
DIFF_HEADS = D_MODEL // (2 * DIFF_HEAD_DIM)
Q_BLOCK = 128
N_BUCKETS = 32
MAX_DISTANCE = 128
D_FF = 4 * D_MODEL
N_EVEN = (DEPTH + 1) // 2
N_ODD = DEPTH // 2
RMS_EPS = 1e-6
LN_EPS = 1e-5
SUBLN_EPS = 1e-5

kernel_name = "hybrid_conv_gmlp_diffattn_adaln_encoder"


def rmsnorm(x, g, eps=RMS_EPS):
    xf = x.astype(jnp.float32)
    y = xf * lax.rsqrt(jnp.mean(xf * xf, axis=-1, keepdims=True) + eps)
    return (y * g.astype(jnp.float32)).astype(x.dtype)


def layernorm(x, g, b, eps=LN_EPS):
    xf = x.astype(jnp.float32)
    mu = jnp.mean(xf, axis=-1, keepdims=True)
    xc = xf - mu
    var = jnp.mean(xc * xc, axis=-1, keepdims=True)
    return (xc * lax.rsqrt(var + eps) * g.astype(jnp.float32) + b.astype(jnp.float32)).astype(x.dtype)


def modulate(h, shift, scale):
    return h * (1.0 + scale[:, None, :]) + shift[:, None, :]


def t5_bucket(rel):
    nb = N_BUCKETS // 2
    max_exact = nb // 2
    ret = jnp.where(rel > 0, nb, 0).astype(jnp.int32)
    n = jnp.abs(rel)
    nf = jnp.maximum(n, 1).astype(jnp.float32)
    large = max_exact + (jnp.log(nf / max_exact) / math.log(MAX_DISTANCE / max_exact)
                         * (nb - max_exact)).astype(jnp.int32)
    large = jnp.minimum(large, nb - 1)
    return ret + jnp.where(n < max_exact, n, large)


def conv_gmlp_mixer(h, w_in, conv_w, conv_b, conv_ln_g, conv_ln_b,
                    sg_ln_g, sg_ln_b, sg_w, sg_b, w_out):
    B, S, _ = h.shape
    z = h @ w_in
    a_val = z[..., :CONV_CH]
    a_gate = z[..., CONV_CH:2 * CONV_CH]
    z_sg = z[..., 2 * CONV_CH:]
    a = a_val * jax.nn.sigmoid(a_gate)
    pad = CONV_WIDTH // 2
    a = lax.conv_general_dilated(
        a, conv_w[:, None, :].astype(a.dtype), window_strides=(1,),
        padding=[(pad, pad)], dimension_numbers=('NWC', 'WIO', 'NWC'),
        feature_group_count=CONV_CH) + conv_b
    a = jax.nn.silu(layernorm(a, conv_ln_g, conv_ln_b))
    zs = jax.nn.gelu(z_sg, approximate=False)
    u = zs[..., :SG_CH]
    v = layernorm(zs[..., SG_CH:], sg_ln_g, sg_ln_b)
    v = v.reshape(B, S // SG_CHUNK, SG_CHUNK, SG_HEADS, SG_HEAD_DIM)
    v = jnp.einsum('gpq,bnqgc->bnpgc', sg_w, v) + sg_b.T[None, None, :, :, None]
    sg = u * v.reshape(B, S, SG_CH)
    return jnp.concatenate([a, sg], axis=-1) @ w_out


def diff_attention(h, w_qkv, lq1, lk1, lq2, lk2, subln_g, w_out, rel_bias, lambda_init):
    B, S, D = h.shape
    H, d = DIFF_HEADS, DIFF_HEAD_DIM
    q, k, v = jnp.split(h @ w_qkv, 3, axis=-1)
    q = q.reshape(B, S, H, 2, d) * (d ** -0.5)
    k = k.reshape(B, S, H, 2, d)
    v = v.reshape(B, S, H, 2 * d)
    f32 = jnp.float32
    lam = (jnp.exp(jnp.sum(lq1.astype(f32) * lk1.astype(f32)))
           - jnp.exp(jnp.sum(lq2.astype(f32) * lk2.astype(f32))) + lambda_init)
    nblk = S // Q_BLOCK
    q_blocks = jnp.moveaxis(q.reshape(B, nblk, Q_BLOCK, H, 2, d), 1, 0)
    key_pos = jnp.arange(S, dtype=jnp.int32)

    def block(args):
        qb, blk = args
        q_pos = blk * Q_BLOCK + jnp.arange(Q_BLOCK, dtype=jnp.int32)
        rel_idx = key_pos[None, :] - q_pos[:, None] + (S - 1)
        bias = jnp.transpose(jnp.take(rel_bias, rel_idx, axis=0), (2, 0, 1))
        logits = (jnp.einsum('bqhjd,bkhjd->bhjqk', qb, k).astype(f32)
                  + bias.astype(f32)[None, :, None])
        p = jax.nn.softmax(logits, axis=-1)
        attn = p[:, :, 0] - lam * p[:, :, 1]
        return jnp.einsum('bhqk,bkhe->bqhe', attn.astype(v.dtype), v)

    o = lax.map(block, (q_blocks, jnp.arange(nblk, dtype=jnp.int32)))
    o = jnp.moveaxis(o, 0, 1).reshape(B, S, H, 2 * d)
    o = rmsnorm(o, subln_g, SUBLN_EPS) * (1.0 - lambda_init)
    return o.reshape(B, S, D) @ w_out


def setup_inputs(seed: int = 0) -> dict:
    key = jax.random.key(seed)
    ks = iter(jax.random.split(key, 40))
    nrm = lambda shape, s: jax.random.normal(next(ks), shape, jnp.float32) * s
    gain = lambda shape: 1.0 + nrm(shape, 0.02)
    D = D_MODEL
    return {
        "x": nrm((BATCH, SEQ, D), 1.0),
        "c": nrm((BATCH, D), 1.0),
        "mix_norm_g": gain((DEPTH, D)),
        "mix_mod_w": nrm((DEPTH, D, 3 * D), 0.5 * D ** -0.5),
        "mix_mod_b": nrm((DEPTH, 3 * D), 0.02),
        "ab_w_in": nrm((N_EVEN, D, IN_AB), D ** -0.5),
        "conv_w": nrm((N_EVEN, CONV_WIDTH, CONV_CH), CONV_WIDTH ** -0.5),
        "conv_b": nrm((N_EVEN, CONV_CH), 0.02),
        "conv_ln_g": gain((N_EVEN, CONV_CH)),
        "conv_ln_b": nrm((N_EVEN, CONV_CH), 0.02),
        "sg_ln_g": gain((N_EVEN, SG_CH)),
        "sg_ln_b": nrm((N_EVEN, SG_CH), 0.02),
        "sg_w": nrm((N_EVEN, SG_HEADS, SG_CHUNK, SG_CHUNK), SG_CHUNK ** -0.5),
        "sg_b": nrm((N_EVEN, SG_HEADS, SG_CHUNK), 0.02),
        "ab_w_out": nrm((N_EVEN, MIX_WIDTH, D), MIX_WIDTH ** -0.5),
        "attn_w_qkv": nrm((N_ODD, D, 3 * D), D ** -0.5),
        "lam_q1": nrm((N_ODD, DIFF_HEAD_DIM), 0.1),
        "lam_k1": nrm((N_ODD, DIFF_HEAD_DIM), 0.1),
        "lam_q2": nrm((N_ODD, DIFF_HEAD_DIM), 0.1),
        "lam_k2": nrm((N_ODD, DIFF_HEAD_DIM), 0.1),
        "subln_g": gain((N_ODD, 2 * DIFF_HEAD_DIM)),
        "attn_w_out": nrm((N_ODD, D, D), D ** -0.5),
        "rel_bias_table": nrm((N_BUCKETS, DIFF_HEADS), 0.2),
        "mlp_norm_g": gain((DEPTH, D)),
        "mlp_mod_w": nrm((DEPTH, D, 3 * D), 0.5 * D ** -0.5),
        "mlp_mod_b": nrm((DEPTH, 3 * D), 0.02),
        "mlp_w1": nrm((DEPTH, D, D_FF), D ** -0.5),
        "mlp_w2": nrm((DEPTH, D_FF, D), D_FF ** -0.5),
        "final_norm_g": gain((D,)),
    }


def reference(x, c, mix_norm_g, mix_mod_w, mix_mod_b, ab_w_in, conv_w, conv_b,
              conv_ln_g, conv_ln_b, sg_ln_g, sg_ln_b, sg_w, sg_b, ab_w_out,
              attn_w_qkv, lam_q1, lam_k1, lam_q2, lam_k2, subln_g, attn_w_out,
              rel_bias_table, mlp_norm_g, mlp_mod_w, mlp_mod_b, mlp_w1, mlp_w2,
              final_norm_g):
    S = x.shape[1]
    c_act = jax.nn.silu(c)
    rel = jnp.arange(-(S - 1), S, dtype=jnp.int32)
    rel_bias = rel_bias_table[t5_bucket(rel)]
    for i in range(DEPTH):
        j = i // 2
        shift, scale, gate = jnp.split(c_act @ mix_mod_w[i] + mix_mod_b[i], 3, axis=-1)
        hdn = modulate(rmsnorm(x, mix_norm_g[i]), shift, scale)
        if i % 2 == 0:
            y = conv_gmlp_mixer(hdn, ab_w_in[j], conv_w[j], conv_b[j], conv_ln_g[j],
                                conv_ln_b[j], sg_ln_g[j], sg_ln_b[j], sg_w[j], sg_b[j],
                                ab_w_out[j])
        else:
            lambda_init = 0.8 - 0.6 * math.exp(-0.3 * i)
            y = diff_attention(hdn, attn_w_qkv[j], lam_q1[j], lam_k1[j], lam_q2[j],
                               lam_k2[j], subln_g[j], attn_w_out[j], rel_bias, lambda_init)
        x = x + gate[:, None, :] * y
        shift, scale, gate = jnp.split(c_act @ mlp_mod_w[i] + mlp_mod_b[i], 3, axis=-1)
        hdn = modulate(rmsnorm(x, mlp_norm_g[i]), shift, scale)
        x = x + gate[:, None, :] * (jnp.square(jax.nn.relu(hdn @ mlp_w1[i])) @ mlp_w2[i])
    return rmsnorm(x, final_norm_g)
```

```python
import functools
import math

import jax
import jax.numpy as jnp
import numpy as np
from jax import lax
from jax.experimental import pallas as pl
from jax.experimental.pallas import tpu as pltpu

F32 = jnp.float32
BF16 = jnp.bfloat16

CONV_WIDTH = 31
SG_HEADS = 4
SG_CHUNK = 128
DIFF_HEAD_DIM = 64
N_BUCKETS = 32
MAX_DISTANCE = 128
RMS_EPS = 1e-6
LN_EPS = 1e-5
SUBLN_EPS = 1e-5

V7X_LANES = 128
V7X_SUBLANES = 8
V7X_VMEM_SCOPED_CAP_BYTES = 60000 * 1024

HALO_ROWS = 16


def _vmem_limit(pipelined_bytes, resident_bytes, temp_bytes):
    est = 2 * pipelined_bytes + resident_bytes + temp_bytes + (4 << 20)
    return int(min(est, V7X_VMEM_SCOPED_CAP_BYTES))


def _resident(block_shape, index_map):
    return pl.BlockSpec(block_shape, index_map, pipeline_mode=pl.Buffered(1))


def _rms_modulate(x, g, shift, scale):
    ms = jnp.mean(x * x, axis=-1, keepdims=True)
    y = x * lax.rsqrt(ms + RMS_EPS) * g
    return y * (1.0 + scale) + shift


def _layernorm(x, g, b):
    mu = jnp.mean(x, axis=-1, keepdims=True)
    xc = x - mu
    var = jnp.mean(xc * xc, axis=-1, keepdims=True)
    return xc * lax.rsqrt(var + LN_EPS) * g + b


def _gelu_exact(x):
    return 0.5 * x * (1.0 + lax.erf(x * np.float32(math.sqrt(0.5))))


def _mod_kernel(c_ref, w_ref, b_ref, o_ref):
    c = c_ref[...]
    ca = c * jax.nn.sigmoid(c)
    o_ref[0] = jnp.dot(ca, w_ref[0], preferred_element_type=F32) + b_ref[0]


def _modulation(c, w, b):
    n_layers, d, d3 = w.shape
    bsz = c.shape[0]
    n_col = d3 // d
    return pl.pallas_call(
        _mod_kernel,
        out_shape=jax.ShapeDtypeStruct((n_layers, bsz, d3), F32),
        grid=(n_layers, n_col),
        in_specs=[
            pl.BlockSpec((bsz, d), lambda i, j: (0, 0)),
            pl.BlockSpec((1, d, d), lambda i, j: (i, 0, j)),
            pl.BlockSpec((1, 1, d), lambda i, j: (i, 0, j)),
        ],
        out_specs=pl.BlockSpec((1, bsz, d), lambda i, j: (i, 0, j)),
        compiler_params=pltpu.CompilerParams(
            dimension_semantics=("parallel", "parallel"),
            vmem_limit_bytes=_vmem_limit(d * d * 4 + 2 * bsz * d * 4, 0, 0)),
        name="adaln_modulation",
    )(c, w, b.reshape(n_layers, 1, d3))


def _mixer_kernel(x_ref, xp_ref, xn_ref, mod_ref, ng_ref, win_ref, cw_ref, cb_ref,
                  clg_ref, clb_ref, slg_ref, slb_ref, sgw_ref, sgb_ref, wout_ref,
                  o_ref, abuf_ref, cbuf_ref, cat_ref, *, tile, d, conv_ch, sg_ch):
    t = pl.program_id(1)
    n_t = pl.num_programs(1)
    mod = mod_ref[0]
    shift, scale, gate = mod[:, :d], mod[:, d:2 * d], mod[:, 2 * d:]
    g = ng_ref[...]
    n_slab = conv_ch // V7X_LANES
    pad = CONV_WIDTH // 2

    x = x_ref[0]
    h = _rms_modulate(x, g, shift, scale).astype(BF16)
    xh = jnp.concatenate([xp_ref[0], xn_ref[0]], axis=0)
    hh = _rms_modulate(xh, g, shift, scale).astype(BF16)

    za = jnp.dot(h, win_ref[:, :2 * conv_ch], preferred_element_type=F32)
    a_main = za[:, :conv_ch] * jax.nn.sigmoid(za[:, conv_ch:])
    zh = jnp.dot(hh, win_ref[:, :2 * conv_ch], preferred_element_type=F32)
    a_halo = zh[:, :conv_ch] * jax.nn.sigmoid(zh[:, conv_ch:])
    a_prev = jnp.where(t > 0, a_halo[:HALO_ROWS], 0.0)
    a_next = jnp.where(t < n_t - 1, a_halo[HALO_ROWS:], 0.0)
    for j in range(n_slab):
        ls = slice(j * V7X_LANES, (j + 1) * V7X_LANES)
        abuf_ref[j, 0:HALO_ROWS, :] = a_prev[:, ls]
        abuf_ref[j, HALO_ROWS:HALO_ROWS + tile, :] = a_main[:, ls]
        abuf_ref[j, HALO_ROWS + tile:, :] = a_next[:, ls]

    rows = 64
    base = HALO_ROWS - pad
    for j in range(n_slab):
        ls = slice(j * V7X_LANES, (j + 1) * V7X_LANES)
        for r0 in range(0, tile, rows):
            acc = jnp.zeros((rows, V7X_LANES), F32) + cb_ref[:, ls]
            for k in range(CONV_WIDTH):
                acc = acc + cw_ref[k:k + 1, ls] * abuf_ref[j, r0 + base + k:r0 + base + k + rows, :]
            cbuf_ref[r0:r0 + rows, ls] = acc
    a = _layernorm(cbuf_ref[...], clg_ref[...], clb_ref[...])
    a = a * jax.nn.sigmoid(a)
    cat_ref[:, :conv_ch] = a.astype(BF16)

    zu = jnp.dot(h, win_ref[:, 2 * conv_ch:2 * conv_ch + sg_ch], preferred_element_type=F32)
    zv = jnp.dot(h, win_ref[:, 2 * conv_ch + sg_ch:], preferred_element_type=F32)
    u = _gelu_exact(zu)
    v = _layernorm(_gelu_exact(zv), slg_ref[...], slb_ref[...]).astype(BF16)
    hd = sg_ch // SG_HEADS
    for n in range(tile // SG_CHUNK):
        rs = slice(n * SG_CHUNK, (n + 1) * SG_CHUNK)
        for gi in range(SG_HEADS):
            cs = slice(gi * hd, (gi + 1) * hd)
            vp = jnp.dot(sgw_ref[gi], v[rs, cs], preferred_element_type=F32) + sgb_ref[gi]
            cat_ref[rs, conv_ch + gi * hd:conv_ch + (gi + 1) * hd] = (u[rs, cs] * vp).astype(BF16)

    y = jnp.dot(cat_ref[...], wout_ref[...], preferred_element_type=F32)
    o_ref[0] = x + gate * y


def _mixer_layer(x, mod, norm_g, w_in, conv_w, conv_b, cln_g, cln_b, sln_g, sln_b,
                 sg_w, sg_b, w_out, *, tile=512):
    bsz, seq, d = x.shape
    conv_ch = conv_w.shape[1]
    sg_ch = sln_g.shape[0]
    in_ab = w_in.shape[1]
    mix = w_out.shape[0]
    assert in_ab == 2 * conv_ch + 2 * sg_ch and mix == conv_ch + sg_ch
    assert seq % tile == 0 and tile % SG_CHUNK == 0 and tile % HALO_ROWS == 0
    assert HALO_ROWS >= CONV_WIDTH // 2 and conv_ch % V7X_LANES == 0
    assert sg_ch // SG_HEADS == V7X_LANES
    n_t = seq // tile
    hb = tile // HALO_ROWS
    n_halo_blocks = seq // HALO_ROWS
    row = lambda a: a.reshape(1, -1)
    kern = functools.partial(_mixer_kernel, tile=tile, d=d, conv_ch=conv_ch, sg_ch=sg_ch)
    const2 = lambda b, t: (0, 0)
    const3 = lambda b, t: (0, 0, 0)
    pipelined = 2 * tile * d * 4 + 2 * HALO_ROWS * d * 4 + 3 * d * 4
    resident = (d * in_ab + mix * d) * 2 + SG_HEADS * SG_CHUNK * SG_CHUNK * 2 + (1 << 20)
    scratch = (conv_ch // V7X_LANES) * (tile + 2 * HALO_ROWS) * V7X_LANES * 4 + tile * conv_ch * 4 + tile * mix * 2
    temps = 6 * tile * in_ab * 4
    return pl.pallas_call(
        kern,
        out_shape=jax.ShapeDtypeStruct((bsz, seq, d), F32),
        grid=(bsz, n_t),
        in_specs=[
            pl.BlockSpec((1, tile, d), lambda b, t: (b, t, 0)),
            pl.BlockSpec((1, HALO_ROWS, d), lambda b, t: (b, jnp.maximum(t * hb - 1, 0), 0)),
            pl.BlockSpec((1, HALO_ROWS, d),
                         lambda b, t: (b, jnp.minimum((t + 1) * hb, n_halo_blocks - 1), 0)),
            pl.BlockSpec((1, 1, 3 * d), lambda b, t: (b, 0, 0)),
            _resident((1, d), const2),
            _resident((d, in_ab), const2),
            _resident((CONV_WIDTH, conv_ch), const2),
            _resident((1, conv_ch), const2),
            _resident((1, conv_ch), const2),
            _resident((1, conv_ch), const2),
            _resident((1, sg_ch), const2),
            _resident((1, sg_ch), const2),
            _resident((SG_HEADS, SG_CHUNK, SG_CHUNK), const3),
            _resident((SG_HEADS, SG_CHUNK, 1), const3),
            _resident((mix, d), const2),
        ],
        out_specs=pl.BlockSpec((1, tile, d), lambda b, t: (b, t, 0)),
        scratch_shapes=[
            pltpu.VMEM((conv_ch // V7X_LANES, tile + 2 * HALO_ROWS, V7X_LANES), F32),
            pltpu.VMEM((tile, conv_ch), F32),
            pltpu.VMEM((tile, mix), BF16),
        ],
        compiler_params=pltpu.CompilerParams(
            dimension_semantics=("parallel", "parallel"),
            vmem_limit_bytes=_vmem_limit(pipelined, resident + scratch, temps)),
        name="conv_gmlp_mixer",
    )(x, x, x, mod.reshape(bsz, 1, 3 * d), row(norm_g), w_in.astype(BF16), conv_w, row(conv_b),
      row(cln_g), row(cln_b), row(sln_g), row(sln_b), sg_w.astype(BF16),
      sg_b.reshape(SG_HEADS, SG_CHUNK, 1), w_out.astype(BF16))


def _mlp_kernel(*refs, d, d_ff, ff_chunk, pre_proj, final_norm):
    it = iter(refs)
    x_ref = next(it)
    if pre_proj:
        attn_ref, pmod_ref, wo_ref = next(it), next(it), next(it)
    mod_ref, ng_ref, w1_ref, w2_ref = next(it), next(it), next(it), next(it)
    fg_ref = next(it) if final_norm else None
    o_ref = next(it)

    x = x_ref[0]
    if pre_proj:
        pgate = pmod_ref[0][:, 2 * d:]
        x = x + pgate * jnp.dot(attn_ref[0], wo_ref[...], preferred_element_type=F32)
    mod = mod_ref[0]
    shift, scale, gate = mod[:, :d], mod[:, d:2 * d], mod[:, 2 * d:]
    h = _rms_modulate(x, ng_ref[...], shift, scale).astype(BF16)
    acc = jnp.zeros(x.shape, F32)
    for c0 in range(0, d_ff, ff_chunk):
        a = jnp.dot(h, w1_ref[:, c0:c0 + ff_chunk], preferred_element_type=F32)
        a = jnp.square(jnp.maximum(a, 0.0)).astype(BF16)
        acc = acc + jnp.dot(a, w2_ref[c0:c0 + ff_chunk, :], preferred_element_type=F32)
    out = x + gate * acc
    if final_norm:
        ms = jnp.mean(out * out, axis=-1, keepdims=True)
        out = out * lax.rsqrt(ms + RMS_EPS) * fg_ref[...]
    o_ref[0] = out


def _mlp_layer(x, mod, norm_g, w1, w2, *, attn=None, attn_mod=None, attn_w_out=None,
               final_g=None, tile=512, ff_chunk=1024):
    bsz, seq, d = x.shape
    d_ff = w1.shape[1]
    assert seq % tile == 0 and d_ff % ff_chunk == 0
    pre_proj = attn is not None
    final_norm = final_g is not None
    const2 = lambda b, t: (0, 0)
    tile_spec = pl.BlockSpec((1, tile, d), lambda b, t: (b, t, 0))
    mod_spec = pl.BlockSpec((1, 1, 3 * d), lambda b, t: (b, 0, 0))
    in_specs, args = [tile_spec], [x]
    pipelined = 2 * tile * d * 4 + 3 * d * 4
    resident = 2 * d * d_ff * 2 + 2 * d * 4
    if pre_proj:
        in_specs += [tile_spec, mod_spec, _resident((d, d), const2)]
        args += [attn, attn_mod.reshape(bsz, 1, 3 * d), attn_w_out.astype(BF16)]
        pipelined += tile * d * 2 + 3 * d * 4
        resident += d * d * 2
    in_specs += [mod_spec, _resident((1, d), const2), _resident((d, d_ff), const2),
                 _resident((d_ff, d), const2)]
    args += [mod.reshape(bsz, 1, 3 * d), norm_g.reshape(1, d), w1.astype(BF16), w2.astype(BF16)]
    if final_norm:
        in_specs.append(_resident((1, d), const2))
        args.append(final_g.reshape(1, d))
    kern = functools.partial(_mlp_kernel, d=d, d_ff=d_ff, ff_chunk=ff_chunk,
                             pre_proj=pre_proj, final_norm=final_norm)
    temps = 4 * tile * d * 4 + 2 * tile * ff_chunk * 4
    return pl.pallas_call(
        kern,
        out_shape=jax.ShapeDtypeStruct((bsz, seq, d), F32),
        grid=(bsz, seq // tile),
        in_specs=in_specs,
        out_specs=tile_spec,
        compiler_params=pltpu.CompilerParams(
            dimension_semantics=("parallel", "parallel"),
            vmem_limit_bytes=_vmem_limit(pipelined, resident, temps)),
        name="relu2_mlp",
    )(*args)


def _qkv_kernel(x_ref, mod_ref, ng_ref, w_ref, q_ref, k1_ref, k2_ref, v_ref, *, d):
    mod = mod_ref[0]
    shift, scale = mod[:, :d], mod[:, d:2 * d]
    h = _rms_modulate(x_ref[0], ng_ref[...], shift, scale).astype(BF16)
    q = jnp.dot(h, w_ref[:, :d], preferred_element_type=F32)
    q_ref[0] = (q * np.float32(DIFF_HEAD_DIM ** -0.5)).astype(BF16)
    k = jnp.dot(h, w_ref[:, d:2 * d], preferred_element_type=F32)
    first = (lax.broadcasted_iota(jnp.int32, k.shape, 1) % (2 * DIFF_HEAD_DIM)) < DIFF_HEAD_DIM
    k1_ref[0] = jnp.where(first, k, 0.0).astype(BF16)
    k2_ref[0] = jnp.where(first, 0.0, k).astype(BF16)
    v_ref[0] = jnp.dot(h, w_ref[:, 2 * d:], preferred_element_type=F32).astype(BF16)


def _qkv_projection(x, mod, norm_g, w_qkv, *, tile=512):
    bsz, seq, d = x.shape
    assert seq % tile == 0 and w_qkv.shape == (d, 3 * d)
    tile_spec = pl.BlockSpec((1, tile, d), lambda b, t: (b, t, 0))
    out = jax.ShapeDtypeStruct((bsz, seq, d), BF16)
    const2 = lambda b, t: (0, 0)
    return pl.pallas_call(
        functools.partial(_qkv_kernel, d=d),
        out_shape=(out, out, out, out),
        grid=(bsz, seq // tile),
        in_specs=[tile_spec, pl.BlockSpec((1, 1, 3 * d), lambda b, t: (b, 0, 0)),
                  _resident((1, d), const2), _resident((d, 3 * d), const2)],
        out_specs=(tile_spec, tile_spec, tile_spec, tile_spec),
        compiler_params=pltpu.CompilerParams(
            dimension_semantics=("parallel", "parallel"),
            vmem_limit_bytes=_vmem_limit(tile * d * 4 + 4 * tile * d * 2 + 3 * d * 4,
                                         3 * d * d * 2, 6 * tile * d * 4)),
        name="qkv_projection",
    )(x, mod.reshape(bsz, 1, 3 * d), norm_g.reshape(1, d), w_qkv.astype(BF16))


def _t5_bucket_table(max_dist):
    nb = N_BUCKETS // 2
    max_exact = nb // 2
    assert (N_BUCKETS, MAX_DISTANCE) == (32, 128)
    thresholds = np.array([8, 12, 16, 23, 32, 46, 64, 91])
    rel = np.arange(-max_dist, max_dist + 1)
    n = np.abs(rel)
    large = max_exact + (n[:, None] >= thresholds[None, :]).sum(axis=1) - 1
    large = np.minimum(large, nb - 1)
    return (np.where(rel > 0, nb, 0) + np.where(n < max_exact, n, large)).astype(np.int32)


FAR_DISTANCE = 91


def _band_kernel(tab_ref, idx_ref, o_ref):
    hd = pl.program_id(0)
    idx = idx_ref[...]
    acc = jnp.zeros(idx.shape, F32)
    for b in range(N_BUCKETS):
        acc = jnp.where(idx == b, tab_ref[b, hd], acc)
    o_ref[0] = acc


def _bias_band(rel_bias_table, tile):
    n_heads = rel_bias_table.shape[1]
    buckets = _t5_bucket_table(2 * tile)
    i = np.arange(tile)[:, None]
    jj = np.arange(3 * tile)[None, :]
    idx = buckets[(jj - tile - i) + 2 * tile]
    return pl.pallas_call(
        _band_kernel,
        out_shape=jax.ShapeDtypeStruct((n_heads, tile, 3 * tile), F32),
        grid=(n_heads,),
        in_specs=[pl.BlockSpec(memory_space=pltpu.SMEM),
                  pl.BlockSpec((tile, 3 * tile), lambda h: (0, 0))],
        out_specs=pl.BlockSpec((1, tile, 3 * tile), lambda h: (h, 0, 0)),
        compiler_params=pltpu.CompilerParams(
            dimension_semantics=("parallel",),
            vmem_limit_bytes=_vmem_limit(2 * tile * 3 * tile * 4, 0, 4 * tile * 3 * tile * 4)),
        name="t5_bias_band",
    )(rel_bias_table, jnp.asarray(idx))


def _attn_kernel(far_ref, q_ref, k1_ref, k2_ref, v_ref, band_ref, lamv_ref, sg_ref, o_ref,
                 m_ref, l_ref, acc_ref, *, tile, n_chunks, lambda_init):
    hd = pl.program_id(1)
    qi = pl.program_id(2)
    q = q_ref[0]
    m_ref[...] = jnp.full(m_ref.shape, -jnp.inf, F32)
    l_ref[...] = jnp.zeros(l_ref.shape, F32)
    acc_ref[...] = jnp.zeros(acc_ref.shape, F32)

    def update(kc, bias_tile, const_bias):
        start = pl.multiple_of(kc * tile, tile)
        v = v_ref[0, pl.ds(start, tile), :]
        for j, k_ref in enumerate((k1_ref, k2_ref)):
            k = k_ref[0, pl.ds(start, tile), :]
            s = lax.dot_general(q, k, (((1,), (1,)), ((), ())), preferred_element_type=F32)
            if bias_tile is not None:
                s = s + bias_tile
            row_max = jnp.max(s, axis=-1, keepdims=True)
            if const_bias is not None:
                row_max = row_max + const_bias
            m_prev = m_ref[j]
            m_new = jnp.maximum(m_prev, row_max)
            alpha = jnp.exp(m_prev - m_new)
            offset = m_new if const_bias is None else m_new - const_bias
            p = jnp.exp(s - offset)
            l_ref[j] = alpha * l_ref[j] + jnp.sum(p, axis=-1, keepdims=True)
            acc_ref[j] = alpha * acc_ref[j] + jnp.dot(p.astype(BF16), v, preferred_element_type=F32)
            m_ref[j] = m_new

    def far_loop(lo, hi, const_bias):
        def body(kc, carry):
            update(kc, None, const_bias)
            return carry
        lax.fori_loop(lo, hi, body, 0)

    far_loop(0, jnp.maximum(qi - 1, 0), far_ref[hd, 0])
    far_loop(qi + 2, n_chunks, far_ref[hd, 1])

    @pl.when(qi >= 1)
    def _():
        update(qi - 1, band_ref[0, :, 0:tile], None)

    update(qi, band_ref[0, :, tile:2 * tile], None)

    @pl.when(qi + 1 < n_chunks)
    def _():
        update(qi + 1, band_ref[0, :, 2 * tile:3 * tile], None)

    lv = lamv_ref[...]
    lam = (jnp.exp(jnp.sum(lv[0:1] * lv[1:2], axis=-1, keepdims=True))
           - jnp.exp(jnp.sum(lv[2:3] * lv[3:4], axis=-1, keepdims=True)) + lambda_init)
    o = acc_ref[0] / l_ref[0] - lam * (acc_ref[1] / l_ref[1])
    ms = jnp.mean(o * o, axis=-1, keepdims=True)
    o = o * lax.rsqrt(ms + SUBLN_EPS) * sg_ref[...] * (1.0 - lambda_init)
    o_ref[0] = o.astype(BF16)


def _diff_attention(q, k1, k2, v, rel_bias_table, lamv, subln_g, lambda_init, *, tile=512):
    bsz, seq, d = q.shape
    hw = 2 * DIFF_HEAD_DIM
    n_heads = d // hw
    assert hw == V7X_LANES and seq % tile == 0 and tile >= FAR_DISTANCE
    n_chunks = seq // tile
    buckets = _t5_bucket_table(seq - 1)
    neg_bucket, pos_bucket = int(buckets[0]), int(buckets[-1])
    assert (buckets[:seq - FAR_DISTANCE] == neg_bucket).all()
    assert (buckets[seq - 1 + FAR_DISTANCE:] == pos_bucket).all()
    far = jnp.stack([rel_bias_table[neg_bucket], rel_bias_table[pos_bucket]], axis=1)
    band = _bias_band(rel_bias_table, tile)
    q_spec = pl.BlockSpec((1, tile, hw), lambda b, h, t: (b, t, h))
    kv_spec = pl.BlockSpec((1, seq, hw), lambda b, h, t: (b, 0, h))
    kern = functools.partial(_attn_kernel, tile=tile, n_chunks=n_chunks, lambda_init=lambda_init)
    pipelined = 2 * tile * hw * 2 + 3 * seq * hw * 2 + tile * 3 * tile * 4
    scratch = 2 * tile * V7X_LANES * 4 * 3
    temps = 8 * tile * tile * 4
    return pl.pallas_call(
        kern,
        out_shape=jax.ShapeDtypeStruct((bsz, seq, d), BF16),
        grid=(bsz, n_heads, n_chunks),
        in_specs=[
            pl.BlockSpec(memory_space=pltpu.SMEM),
            q_spec, kv_spec, kv_spec, kv_spec,
            pl.BlockSpec((1, tile, 3 * tile), lambda b, h, t: (h, 0, 0)),
            pl.BlockSpec((4, DIFF_HEAD_DIM), lambda b, h, t: (0, 0)),
            pl.BlockSpec((1, hw), lambda b, h, t: (0, 0)),
        ],
        out_specs=q_spec,
        scratch_shapes=[
            pltpu.VMEM((2, tile, 1), F32),
            pltpu.VMEM((2, tile, 1), F32),
            pltpu.VMEM((2, tile, hw), F32),
        ],
        compiler_params=pltpu.CompilerParams(
            dimension_semantics=("parallel", "parallel", "parallel"),
            vmem_limit_bytes=_vmem_limit(pipelined, scratch, temps)),
        name="diff_flash_attention",
    )(far, q, k1, k2, v, band, lamv, subln_g.reshape(1, hw))


def kernel(x, c, mix_norm_g, mix_mod_w, mix_mod_b, ab_w_in, conv_w, conv_b, conv_ln_g, conv_ln_b, sg_ln_g, sg_ln_b, sg_w, sg_b, ab_w_out, attn_w_qkv, lam_q1, lam_k1, lam_q2, lam_k2, subln_g, attn_w_out, rel_bias_table, mlp_norm_g, mlp_mod_w, mlp_mod_b, mlp_w1, mlp_w2, final_norm_g):
    depth = mix_norm_g.shape[0]
    mix_mod = _modulation(c, mix_mod_w, mix_mod_b)
    mlp_mod = _modulation(c, mlp_mod_w, mlp_mod_b)
    for i in range(depth):
        j = i // 2
        last = final_norm_g if i == depth - 1 else None
        if i % 2 == 0:
            x = _mixer_layer(x, mix_mod[i], mix_norm_g[i], ab_w_in[j], conv_w[j], conv_b[j],
                             conv_ln_g[j], conv_ln_b[j], sg_ln_g[j], sg_ln_b[j], sg_w[j], sg_b[j],
                             ab_w_out[j])
            x = _mlp_layer(x, mlp_mod[i], mlp_norm_g[i], mlp_w1[i], mlp_w2[i], final_g=last)
        else:
            lambda_init = 0.8 - 0.6 * math.exp(-0.3 * i)
            q, k1, k2, v = _qkv_projection(x, mix_mod[i], mix_norm_g[i], attn_w_qkv[j])
            lamv = jnp.stack([lam_q1[j], lam_k1[j], lam_q2[j], lam_k2[j]])
            o = _diff_attention(q, k1, k2, v, rel_bias_table, lamv, subln_g[j], lambda_init)
            x = _mlp_layer(x, mlp_mod[i], mlp_norm_g[i], mlp_w1[i], mlp_w2[i], attn=o,
                           attn_mod=mix_mod[i], attn_w_out=attn_w_out[j], final_g=last)
    return x
```

```python
import functools
import math

import jax
import jax.numpy as jnp
import numpy as np
from jax import lax
from jax.experimental import pallas as pl
from jax.experimental.pallas import tpu as pltpu

F32 = jnp.float32
BF16 = jnp.bfloat16

CONV_WIDTH = 31
SG_HEADS = 4
SG_CHUNK = 128
DIFF_HEAD_DIM = 64
N_BUCKETS = 32
MAX_DISTANCE = 128
RMS_EPS = 1e-6
LN_EPS = 1e-5
SUBLN_EPS = 1e-5
LOG2E = math.log2(math.e)

V7X_LANES = 128
V7X_SUBLANES = 8
V7X_VMEM_SCOPED_CAP_BYTES = 60000 * 1024

HALO_ROWS = 16


def _vmem_limit(pipelined_bytes, resident_bytes, temp_bytes):
    est = 2 * pipelined_bytes + resident_bytes + temp_bytes + (4 << 20)
    return int(min(est, V7X_VMEM_SCOPED_CAP_BYTES))


def _resident(block_shape, index_map):
    return pl.BlockSpec(block_shape, index_map, pipeline_mode=pl.Buffered(1))


def _rms_modulate(x, g, shift, scale):
    ms = jnp.mean(x * x, axis=-1, keepdims=True)
    y = x * lax.rsqrt(ms + RMS_EPS) * g
    return y * (1.0 + scale) + shift


def _layernorm(x, g, b):
    mu = jnp.mean(x, axis=-1, keepdims=True)
    xc = x - mu
    var = jnp.mean(xc * xc, axis=-1, keepdims=True)
    return xc * lax.rsqrt(var + LN_EPS) * g + b


def _gelu_exact(x):
    return 0.5 * x * (1.0 + lax.erf(x * np.float32(math.sqrt(0.5))))


def _mod_kernel(c_ref, w_ref, b_ref, o_ref):
    c = c_ref[...]
    ca = c * jax.nn.sigmoid(c)
    o_ref[0] = jnp.dot(ca, w_ref[0], preferred_element_type=F32) + b_ref[0]


def _modulation(c, w, b):
    n_layers, d, d3 = w.shape
    bsz = c.shape[0]
    n_col = d3 // d
    return pl.pallas_call(
        _mod_kernel,
        out_shape=jax.ShapeDtypeStruct((n_layers, bsz, d3), F32),
        grid=(n_layers, n_col),
        in_specs=[
            pl.BlockSpec((bsz, d), lambda i, j: (0, 0)),
            pl.BlockSpec((1, d, d), lambda i, j: (i, 0, j)),
            pl.BlockSpec((1, 1, d), lambda i, j: (i, 0, j)),
        ],
        out_specs=pl.BlockSpec((1, bsz, d), lambda i, j: (i, 0, j)),
        compiler_params=pltpu.CompilerParams(
            dimension_semantics=("parallel", "parallel"),
            vmem_limit_bytes=_vmem_limit(d * d * 4 + 2 * bsz * d * 4, 0, 0)),
        name="adaln_modulation",
    )(c, w, b.reshape(n_layers, 1, d3))


def _mixer_kernel(x_ref, xp_ref, xn_ref, mod_ref, ng_ref, win_ref, cw_ref, cb_ref,
                  clg_ref, clb_ref, slg_ref, slb_ref, sgw_ref, sgb_ref, wout_ref,
                  o_ref, abuf_ref, cbuf_ref, cat_ref, *, tile, d, conv_ch, sg_ch):
    t = pl.program_id(1)
    n_t = pl.num_programs(1)
    mod = mod_ref[0]
    shift, scale, gate = mod[:, :d], mod[:, d:2 * d], mod[:, 2 * d:]
    g = ng_ref[...]
    n_slab = conv_ch // V7X_LANES
    pad = CONV_WIDTH // 2

    x = x_ref[0]
    h = _rms_modulate(x, g, shift, scale).astype(BF16)
    xh = jnp.concatenate([xp_ref[0], xn_ref[0]], axis=0)
    hh = _rms_modulate(xh, g, shift, scale).astype(BF16)

    za = jnp.dot(h, win_ref[:, :2 * conv_ch], preferred_element_type=F32)
    a_main = za[:, :conv_ch] * jax.nn.sigmoid(za[:, conv_ch:])
    zh = jnp.dot(hh, win_ref[:, :2 * conv_ch], preferred_element_type=F32)
    a_halo = zh[:, :conv_ch] * jax.nn.sigmoid(zh[:, conv_ch:])
    a_prev = jnp.where(t > 0, a_halo[:HALO_ROWS], 0.0)
    a_next = jnp.where(t < n_t - 1, a_halo[HALO_ROWS:], 0.0)
    for j in range(n_slab):
        ls = slice(j * V7X_LANES, (j + 1) * V7X_LANES)
        abuf_ref[j, 0:HALO_ROWS, :] = a_prev[:, ls]
        abuf_ref[j, HALO_ROWS:HALO_ROWS + tile, :] = a_main[:, ls]
        abuf_ref[j, HALO_ROWS + tile:, :] = a_next[:, ls]

    rows = 64
    base = HALO_ROWS - pad
    for j in range(n_slab):
        ls = slice(j * V7X_LANES, (j + 1) * V7X_LANES)
        for r0 in range(0, tile, rows):
            acc = jnp.zeros((rows, V7X_LANES), F32) + cb_ref[:, ls]
            for k in range(CONV_WIDTH):
                acc = acc + cw_ref[k:k + 1, ls] * abuf_ref[j, r0 + base + k:r0 + base + k + rows, :]
            cbuf_ref[r0:r0 + rows, ls] = acc
    a = _layernorm(cbuf_ref[...], clg_ref[...], clb_ref[...])
    a = a * jax.nn.sigmoid(a)
    cat_ref[:, :conv_ch] = a.astype(BF16)

    zu = jnp.dot(h, win_ref[:, 2 * conv_ch:2 * conv_ch + sg_ch], preferred_element_type=F32)
    zv = jnp.dot(h, win_ref[:, 2 * conv_ch + sg_ch:], preferred_element_type=F32)
    u = _gelu_exact(zu)
    v = _layernorm(_gelu_exact(zv), slg_ref[...], slb_ref[...]).astype(BF16)
    hd = sg_ch // SG_HEADS
    for n in range(tile // SG_CHUNK):
        rs = slice(n * SG_CHUNK, (n + 1) * SG_CHUNK)
        for gi in range(SG_HEADS):
            cs = slice(gi * hd, (gi + 1) * hd)
            vp = jnp.dot(sgw_ref[gi], v[rs, cs], preferred_element_type=F32) + sgb_ref[gi]
            cat_ref[rs, conv_ch + gi * hd:conv_ch + (gi + 1) * hd] = (u[rs, cs] * vp).astype(BF16)

    y = jnp.dot(cat_ref[...], wout_ref[...], preferred_element_type=F32)
    o_ref[0] = x + gate * y


def _mixer_layer(x, mod, norm_g, w_in, conv_w, conv_b, cln_g, cln_b, sln_g, sln_b,
                 sg_w, sg_b, w_out, *, tile=512):
    bsz, seq, d = x.shape
    conv_ch = conv_w.shape[1]
    sg_ch = sln_g.shape[0]
    in_ab = w_in.shape[1]
    mix = w_out.shape[0]
    assert in_ab == 2 * conv_ch + 2 * sg_ch and mix == conv_ch + sg_ch
    assert seq % tile == 0 and tile % SG_CHUNK == 0 and tile % HALO_ROWS == 0
    assert HALO_ROWS >= CONV_WIDTH // 2 and conv_ch % V7X_LANES == 0
    assert sg_ch // SG_HEADS == V7X_LANES
    n_t = seq // tile
    hb = tile // HALO_ROWS
    n_halo_blocks = seq // HALO_ROWS
    row = lambda a: a.reshape(1, -1)
    kern = functools.partial(_mixer_kernel, tile=tile, d=d, conv_ch=conv_ch, sg_ch=sg_ch)
    const2 = lambda b, t: (0, 0)
    const3 = lambda b, t: (0, 0, 0)
    pipelined = 2 * tile * d * 4 + 2 * HALO_ROWS * d * 4 + 3 * d * 4
    resident = (d * in_ab + mix * d) * 2 + SG_HEADS * SG_CHUNK * SG_CHUNK * 2 + (1 << 20)
    scratch = (conv_ch // V7X_LANES) * (tile + 2 * HALO_ROWS) * V7X_LANES * 4 + tile * conv_ch * 4 + tile * mix * 2
    temps = 6 * tile * in_ab * 4
    return pl.pallas_call(
        kern,
        out_shape=jax.ShapeDtypeStruct((bsz, seq, d), F32),
        grid=(bsz, n_t),
        in_specs=[
            pl.BlockSpec((1, tile, d), lambda b, t: (b, t, 0)),
            pl.BlockSpec((1, HALO_ROWS, d), lambda b, t: (b, jnp.maximum(t * hb - 1, 0), 0)),
            pl.BlockSpec((1, HALO_ROWS, d),
                         lambda b, t: (b, jnp.minimum((t + 1) * hb, n_halo_blocks - 1), 0)),
            pl.BlockSpec((1, 1, 3 * d), lambda b, t: (b, 0, 0)),
            _resident((1, d), const2),
            _resident((d, in_ab), const2),
            _resident((CONV_WIDTH, conv_ch), const2),
            _resident((1, conv_ch), const2),
            _resident((1, conv_ch), const2),
            _resident((1, conv_ch), const2),
            _resident((1, sg_ch), const2),
            _resident((1, sg_ch), const2),
            _resident((SG_HEADS, SG_CHUNK, SG_CHUNK), const3),
            _resident((SG_HEADS, SG_CHUNK, 1), const3),
            _resident((mix, d), const2),
        ],
        out_specs=pl.BlockSpec((1, tile, d), lambda b, t: (b, t, 0)),
        scratch_shapes=[
            pltpu.VMEM((conv_ch // V7X_LANES, tile + 2 * HALO_ROWS, V7X_LANES), F32),
            pltpu.VMEM((tile, conv_ch), F32),
            pltpu.VMEM((tile, mix), BF16),
        ],
        compiler_params=pltpu.CompilerParams(
            dimension_semantics=("parallel", "parallel"),
            vmem_limit_bytes=_vmem_limit(pipelined, resident + scratch, temps)),
        name="conv_gmlp_mixer",
    )(x, x, x, mod.reshape(bsz, 1, 3 * d), row(norm_g), w_in.astype(BF16), conv_w, row(conv_b),
      row(cln_g), row(cln_b), row(sln_g), row(sln_b), sg_w.astype(BF16),
      sg_b.reshape(SG_HEADS, SG_CHUNK, 1), w_out.astype(BF16))


def _mlp_kernel(*refs, d, d_ff, ff_chunk, pre_proj, final_norm):
    it = iter(refs)
    x_ref = next(it)
    if pre_proj:
        attn_ref, pmod_ref, wo_ref = next(it), next(it), next(it)
    mod_ref, ng_ref, w1_ref, w2_ref = next(it), next(it), next(it), next(it)
    fg_ref = next(it) if final_norm else None
    o_ref = next(it)

    x = x_ref[0]
    if pre_proj:
        pgate = pmod_ref[0][:, 2 * d:]
        x = x + pgate * jnp.dot(attn_ref[0], wo_ref[...], preferred_element_type=F32)
    mod = mod_ref[0]
    shift, scale, gate = mod[:, :d], mod[:, d:2 * d], mod[:, 2 * d:]
    h = _rms_modulate(x, ng_ref[...], shift, scale).astype(BF16)
    acc = jnp.zeros(x.shape, F32)
    for c0 in range(0, d_ff, ff_chunk):
        a = jnp.dot(h, w1_ref[:, c0:c0 + ff_chunk], preferred_element_type=F32)
        a = jnp.square(jnp.maximum(a, 0.0)).astype(BF16)
        acc = acc + jnp.dot(a, w2_ref[c0:c0 + ff_chunk, :], preferred_element_type=F32)
    out = x + gate * acc
    if final_norm:
        ms = jnp.mean(out * out, axis=-1, keepdims=True)
        out = out * lax.rsqrt(ms + RMS_EPS) * fg_ref[...]
    o_ref[0] = out


def _mlp_layer(x, mod, norm_g, w1, w2, *, attn=None, attn_mod=None, attn_w_out=None,
               final_g=None, tile=512, ff_chunk=1024):
    bsz, seq, d = x.shape
    d_ff = w1.shape[1]
    assert seq % tile == 0 and d_ff % ff_chunk == 0
    pre_proj = attn is not None
    final_norm = final_g is not None
    const2 = lambda b, t: (0, 0)
    tile_spec = pl.BlockSpec((1, tile, d), lambda b, t: (b, t, 0))
    mod_spec = pl.BlockSpec((1, 1, 3 * d), lambda b, t: (b, 0, 0))
    in_specs, args = [tile_spec], [x]
    pipelined = 2 * tile * d * 4 + 3 * d * 4
    resident = 2 * d * d_ff * 2 + 2 * d * 4
    if pre_proj:
        in_specs += [tile_spec, mod_spec, _resident((d, d), const2)]
        args += [attn, attn_mod.reshape(bsz, 1, 3 * d), attn_w_out.astype(BF16)]
        pipelined += tile * d * 2 + 3 * d * 4
        resident += d * d * 2
    in_specs += [mod_spec, _resident((1, d), const2), _resident((d, d_ff), const2),
                 _resident((d_ff, d), const2)]
    args += [mod.reshape(bsz, 1, 3 * d), norm_g.reshape(1, d), w1.astype(BF16), w2.astype(BF16)]
    if final_norm:
        in_specs.append(_resident((1, d), const2))
        args.append(final_g.reshape(1, d))
    kern = functools.partial(_mlp_kernel, d=d, d_ff=d_ff, ff_chunk=ff_chunk,
                             pre_proj=pre_proj, final_norm=final_norm)
    temps = 4 * tile * d * 4 + 2 * tile * ff_chunk * 4
    return pl.pallas_call(
        kern,
        out_shape=jax.ShapeDtypeStruct((bsz, seq, d), F32),
        grid=(bsz, seq // tile),
        in_specs=in_specs,
        out_specs=tile_spec,
        compiler_params=pltpu.CompilerParams(
            dimension_semantics=("parallel", "parallel"),
            vmem_limit_bytes=_vmem_limit(pipelined, resident, temps)),
        name="relu2_mlp",
    )(*args)


def _qkv_kernel(x_ref, mod_ref, ng_ref, w_ref, q_ref, k1_ref, k2_ref, v_ref, kmax_ref, *, d):
    mod = mod_ref[0]
    shift, scale = mod[:, :d], mod[:, d:2 * d]
    h = _rms_modulate(x_ref[0], ng_ref[...], shift, scale).astype(BF16)
    q = jnp.dot(h, w_ref[:, :d], preferred_element_type=F32)
    q_ref[0] = (q * np.float32(DIFF_HEAD_DIM ** -0.5 * LOG2E)).astype(BF16)
    k = jnp.dot(h, w_ref[:, d:2 * d], preferred_element_type=F32)
    first = (lax.broadcasted_iota(jnp.int32, k.shape, 1) % (2 * DIFF_HEAD_DIM)) < DIFF_HEAD_DIM
    k1_ref[0] = jnp.where(first, k, 0.0).astype(BF16)
    k2_ref[0] = jnp.where(first, 0.0, k).astype(BF16)
    kmax_ref[0, 0] = jnp.max(jnp.abs(k), axis=0, keepdims=True)
    v_ref[0] = jnp.dot(h, w_ref[:, 2 * d:], preferred_element_type=F32).astype(BF16)


def _qkv_projection(x, mod, norm_g, w_qkv, *, tile=512):
    bsz, seq, d = x.shape
    assert seq % tile == 0 and w_qkv.shape == (d, 3 * d)
    n_t = seq // tile
    tile_spec = pl.BlockSpec((1, tile, d), lambda b, t: (b, t, 0))
    out = jax.ShapeDtypeStruct((bsz, seq, d), BF16)
    const2 = lambda b, t: (0, 0)
    q, k1, k2, v, kmax = pl.pallas_call(
        functools.partial(_qkv_kernel, d=d),
        out_shape=(out, out, out, out, jax.ShapeDtypeStruct((bsz, n_t, 1, d), F32)),
        grid=(bsz, n_t),
        in_specs=[tile_spec, pl.BlockSpec((1, 1, 3 * d), lambda b, t: (b, 0, 0)),
                  _resident((1, d), const2), _resident((d, 3 * d), const2)],
        out_specs=(tile_spec, tile_spec, tile_spec, tile_spec,
                   pl.BlockSpec((1, 1, 1, d), lambda b, t: (b, t, 0, 0))),
        compiler_params=pltpu.CompilerParams(
            dimension_semantics=("parallel", "parallel"),
            vmem_limit_bytes=_vmem_limit(tile * d * 4 + 4 * tile * d * 2 + 4 * d * 4,
                                         3 * d * d * 2, 6 * tile * d * 4)),
        name="qkv_projection",
    )(x, mod.reshape(bsz, 1, 3 * d), norm_g.reshape(1, d), w_qkv.astype(BF16))
    return q, k1, k2, v, kmax.reshape(bsz, n_t, d)


def _t5_bucket_table(max_dist):
    nb = N_BUCKETS // 2
    max_exact = nb // 2
    assert (N_BUCKETS, MAX_DISTANCE) == (32, 128)
    thresholds = np.array([8, 12, 16, 23, 32, 46, 64, 91])
    rel = np.arange(-max_dist, max_dist + 1)
    n = np.abs(rel)
    large = max_exact + (n[:, None] >= thresholds[None, :]).sum(axis=1) - 1
    large = np.minimum(large, nb - 1)
    return (np.where(rel > 0, nb, 0) + np.where(n < max_exact, n, large)).astype(np.int32)


FAR_DISTANCE = 91


def _band_kernel(tab_ref, idx_ref, o_ref):
    hd = pl.program_id(0)
    idx = idx_ref[...]
    acc = jnp.zeros(idx.shape, F32)
    for b in range(N_BUCKETS):
        acc = jnp.where(idx == b, tab_ref[b, hd], acc)
    o_ref[0] = acc * np.float32(LOG2E)


def _bias_band(rel_bias_table, tile):
    n_heads = rel_bias_table.shape[1]
    buckets = _t5_bucket_table(2 * tile)
    i = np.arange(tile)[:, None]
    jj = np.arange(3 * tile)[None, :]
    idx = buckets[(jj - tile - i) + 2 * tile]
    return pl.pallas_call(
        _band_kernel,
        out_shape=jax.ShapeDtypeStruct((n_heads, tile, 3 * tile), F32),
        grid=(n_heads,),
        in_specs=[pl.BlockSpec(memory_space=pltpu.SMEM),
                  pl.BlockSpec((tile, 3 * tile), lambda h: (0, 0))],
        out_specs=pl.BlockSpec((1, tile, 3 * tile), lambda h: (h, 0, 0)),
        compiler_params=pltpu.CompilerParams(
            dimension_semantics=("parallel",),
            vmem_limit_bytes=_vmem_limit(2 * tile * 3 * tile * 4, 0, 4 * tile * 3 * tile * 4)),
        name="t5_bias_band",
    )(rel_bias_table, jnp.asarray(idx))


_NT = (((1,), (1,)), ((), ()))
L_MIN, L_MAX = 1e-30, 1e30


def _attn_kernel(far_ref, q_ref, k1_ref, k2_ref, v_ref, kmax_ref, band_ref, lamv_ref, sg_ref, o_ref,
                 m_ref, l_ref, acc_ref, lw_ref, *, tile, n_chunks, lambda_init):
    hd = pl.program_id(1)
    qi = pl.program_id(2)
    q = q_ref[0]
    k_refs = (k1_ref, k2_ref)
    n_blk = tile // V7X_LANES
    c_neg, c_pos, b_max = far_ref[hd, 0], far_ref[hd, 1], far_ref[hd, 2]

    def chunk_operands(kc):
        start = pl.multiple_of(kc * tile, tile)
        return [k_ref[0, pl.ds(start, tile), :] for k_ref in k_refs], v_ref[0, pl.ds(start, tile), :]

    kabs = jnp.max(kmax_ref[0], axis=0, keepdims=True) * np.float32(1.0 + 2.0 ** -6)
    lane = lax.broadcasted_iota(jnp.int32, kabs.shape, 1)
    qa = jnp.abs(q)
    offsets = []
    for j in range(2):
        mine = (lane < DIFF_HEAD_DIM) if j == 0 else (lane >= DIFF_HEAD_DIM)
        kb = jnp.broadcast_to(jnp.where(mine, kabs, 0.0), (V7X_LANES, V7X_LANES)).astype(BF16)
        offsets.append(lax.dot_general(qa, kb, _NT, preferred_element_type=F32) + b_max)
    lw_ref[...] = jnp.zeros(lw_ref.shape, F32)
    acc_ref[...] = jnp.zeros(acc_ref.shape, F32)

    def fast_update(kc, band_off, offs):
        ks, v = chunk_operands(kc)
        for j in range(2):
            s = lax.dot_general(q, ks[j], _NT, preferred_element_type=F32)
            lsum = None
            ps = []
            for jb in range(n_blk):
                sj = s[:, jb * V7X_LANES:(jb + 1) * V7X_LANES]
                if band_off is not None:
                    c0 = band_off + jb * V7X_LANES
                    sj = sj + band_ref[0, :, c0:c0 + V7X_LANES]
                pj = jnp.exp2(sj - offs[j])
                lsum = pj if lsum is None else lsum + pj
                ps.append(pj.astype(BF16))
            lw_ref[j] += lsum
            acc_ref[j] += jnp.dot(jnp.concatenate(ps, axis=1), v, preferred_element_type=F32)

    def fast_far(lo, hi, const_bias):
        offs = [o - const_bias for o in offsets]

        def body(kc, carry):
            fast_update(kc, None, offs)
            return carry
        lax.fori_loop(lo, hi, body, 0)

    fast_far(0, jnp.maximum(qi - 1, 0), c_neg)
    fast_far(qi + 2, n_chunks, c_pos)

    @pl.when(qi >= 1)
    def _():
        fast_update(qi - 1, 0, offsets)

    fast_update(qi, tile, offsets)

    @pl.when(qi + 1 < n_chunks)
    def _():
        fast_update(qi + 1, 2 * tile, offsets)

    l_fast = jnp.sum(lw_ref[...], axis=-1, keepdims=True)
    l_ref[...] = l_fast
    redo = jnp.logical_not(jnp.logical_and(jnp.min(l_fast) >= L_MIN, jnp.max(l_fast) <= L_MAX))

    @pl.when(redo)
    def _():
        m_ref[...] = jnp.full(m_ref.shape, -jnp.inf, F32)
        l_ref[...] = jnp.zeros(l_ref.shape, F32)
        acc_ref[...] = jnp.zeros(acc_ref.shape, F32)

        def update(kc, band_off, const_bias):
            ks, v = chunk_operands(kc)
            for j in range(2):
                s = lax.dot_general(q, ks[j], _NT, preferred_element_type=F32)
                if band_off is not None:
                    s = s + band_ref[0, :, band_off:band_off + tile]
                row_max = jnp.max(s, axis=-1, keepdims=True)
                if const_bias is not None:
                    row_max = row_max + const_bias
                m_prev = m_ref[j]
                m_new = jnp.maximum(m_prev, row_max)
                alpha = jnp.exp2(m_prev - m_new)
                offset = m_new if const_bias is None else m_new - const_bias
                p = jnp.exp2(s - offset)
                l_ref[j] = alpha * l_ref[j] + jnp.sum(p, axis=-1, keepdims=True)
                acc_ref[j] = alpha * acc_ref[j] + jnp.dot(p.astype(BF16), v, preferred_element_type=F32)
                m_ref[j] = m_new

        def far(lo, hi, const_bias):
            def body(kc, carry):
                update(kc, None, const_bias)
                return carry
            lax.fori_loop(lo, hi, body, 0)

        far(0, jnp.maximum(qi - 1, 0), c_neg)
        far(qi + 2, n_chunks, c_pos)

        @pl.when(qi >= 1)
        def _():
            update(qi - 1, 0, None)

        update(qi, tile, None)

        @pl.when(qi + 1 < n_chunks)
        def _():
            update(qi + 1, 2 * tile, None)

    lv = lamv_ref[...]
    lam = (jnp.exp(jnp.sum(lv[0:1] * lv[1:2], axis=-1, keepdims=True))
           - jnp.exp(jnp.sum(lv[2:3] * lv[3:4], axis=-1, keepdims=True)) + lambda_init)
    o = acc_ref[0] / l_ref[0] - lam * (acc_ref[1] / l_ref[1])
    ms = jnp.mean(o * o, axis=-1, keepdims=True)
    o = o * lax.rsqrt(ms + SUBLN_EPS) * sg_ref[...] * (1.0 - lambda_init)
    o_ref[0] = o.astype(BF16)


def _diff_attention(q, k1, k2, v, kmax, rel_bias_table, lamv, subln_g, lambda_init, *, tile=512):
    bsz, seq, d = q.shape
    hw = 2 * DIFF_HEAD_DIM
    n_heads = d // hw
    assert hw == V7X_LANES and seq % tile == 0 and tile >= FAR_DISTANCE
    n_chunks = seq // tile
    n_kmax = kmax.shape[1]
    buckets = _t5_bucket_table(seq - 1)
    neg_bucket, pos_bucket = int(buckets[0]), int(buckets[-1])
    assert (buckets[:seq - FAR_DISTANCE] == neg_bucket).all()
    assert (buckets[seq - 1 + FAR_DISTANCE:] == pos_bucket).all()
    far = jnp.stack([rel_bias_table[neg_bucket], rel_bias_table[pos_bucket],
                     jnp.max(rel_bias_table, axis=0)], axis=1) * np.float32(LOG2E)
    band = _bias_band(rel_bias_table, tile)
    q_spec = pl.BlockSpec((1, tile, hw), lambda b, h, t: (b, t, h))
    kv_spec = pl.BlockSpec((1, seq, hw), lambda b, h, t: (b, 0, h))
    kern = functools.partial(_attn_kernel, tile=tile, n_chunks=n_chunks, lambda_init=lambda_init)
    pipelined = 2 * tile * hw * 2 + 3 * seq * hw * 2 + tile * 3 * tile * 4 + n_kmax * hw * 4
    scratch = 2 * tile * V7X_LANES * 4 * 4
    temps = 8 * tile * tile * 4
    return pl.pallas_call(
        kern,
        out_shape=jax.ShapeDtypeStruct((bsz, seq, d), BF16),
        grid=(bsz, n_heads, n_chunks),
        in_specs=[
            pl.BlockSpec(memory_space=pltpu.SMEM),
            q_spec, kv_spec, kv_spec, kv_spec,
            pl.BlockSpec((1, n_kmax, hw), lambda b, h, t: (b, 0, h)),
            pl.BlockSpec((1, tile, 3 * tile), lambda b, h, t: (h, 0, 0)),
            pl.BlockSpec((4, DIFF_HEAD_DIM), lambda b, h, t: (0, 0)),
            pl.BlockSpec((1, hw), lambda b, h, t: (0, 0)),
        ],
        out_specs=q_spec,
        scratch_shapes=[
            pltpu.VMEM((2, tile, 1), F32),
            pltpu.VMEM((2, tile, 1), F32),
            pltpu.VMEM((2, tile, hw), F32),
            pltpu.VMEM((2, tile, hw), F32),
        ],
        compiler_params=pltpu.CompilerParams(
            dimension_semantics=("parallel", "parallel", "parallel"),
            vmem_limit_bytes=_vmem_limit(pipelined, scratch, temps)),
        name="diff_flash_attention",
    )(far, q, k1, k2, v, kmax, band, lamv, subln_g.reshape(1, hw))


def kernel(x, c, mix_norm_g, mix_mod_w, mix_mod_b, ab_w_in, conv_w, conv_b, conv_ln_g, conv_ln_b, sg_ln_g, sg_ln_b, sg_w, sg_b, ab_w_out, attn_w_qkv, lam_q1, lam_k1, lam_q2, lam_k2, subln_g, attn_w_out, rel_bias_table, mlp_norm_g, mlp_mod_w, mlp_mod_b, mlp_w1, mlp_w2, final_norm_g):
    depth = mix_norm_g.shape[0]
    mix_mod = _modulation(c, mix_mod_w, mix_mod_b)
    mlp_mod = _modulation(c, mlp_mod_w, mlp_mod_b)
    for i in range(depth):
        j = i // 2
        last = final_norm_g if i == depth - 1 else None
        if i % 2 == 0:
            x = _mixer_layer(x, mix_mod[i], mix_norm_g[i], ab_w_in[j], conv_w[j], conv_b[j],
                             conv_ln_g[j], conv_ln_b[j], sg_ln_g[j], sg_ln_b[j], sg_w[j], sg_b[j],
                             ab_w_out[j])
            x = _mlp_layer(x, mlp_mod[i], mlp_norm_g[i], mlp_w1[i], mlp_w2[i], final_g=last)
        else:
            lambda_init = 0.8 - 0.6 * math.exp(-0.3 * i)
            q, k1, k2, v, kmax = _qkv_projection(x, mix_mod[i], mix_norm_g[i], attn_w_qkv[j])
            lamv = jnp.stack([lam_q1[j], lam_k1[j], lam_q2[j], lam_k2[j]])
            o = _diff_attention(q, k1, k2, v, kmax, rel_bias_table, lamv, subln_g[j], lambda_init)
            x = _mlp_layer(x, mlp_mod[i], mlp_norm_g[i], mlp_w1[i], mlp_w2[i], attn=o,
                           attn_mod=mix_mod[i], attn_w_out=attn_w_out[j], final_g=last)
    return x
```

```python
import functools
import math

import jax
import jax.numpy as jnp
import numpy as np
from jax import lax
from jax.experimental import pallas as pl
from jax.experimental.pallas import tpu as pltpu

F32 = jnp.float32
BF16 = jnp.bfloat16

CONV_WIDTH = 31
SG_HEADS = 4
SG_CHUNK = 128
DIFF_HEAD_DIM = 64
N_BUCKETS = 32
MAX_DISTANCE = 128
RMS_EPS = 1e-6
LN_EPS = 1e-5
SUBLN_EPS = 1e-5
LOG2E = math.log2(math.e)

V7X_LANES = 128
V7X_SUBLANES = 8
V7X_VMEM_SCOPED_CAP_BYTES = 60000 * 1024

HALO_ROWS = 16


def _vmem_limit(pipelined_bytes, resident_bytes, temp_bytes):
    est = 2 * pipelined_bytes + resident_bytes + temp_bytes + (4 << 20)
    return int(min(est, V7X_VMEM_SCOPED_CAP_BYTES))


def _resident(block_shape, index_map):
    return pl.BlockSpec(block_shape, index_map, pipeline_mode=pl.Buffered(1))


def _rms_modulate(x, g, shift, scale):
    ms = jnp.mean(x * x, axis=-1, keepdims=True)
    y = x * lax.rsqrt(ms + RMS_EPS) * g
    return y * (1.0 + scale) + shift


def _layernorm(x, g, b):
    mu = jnp.mean(x, axis=-1, keepdims=True)
    xc = x - mu
    var = jnp.mean(xc * xc, axis=-1, keepdims=True)
    return xc * lax.rsqrt(var + LN_EPS) * g + b


def _gelu_exact(x):
    return 0.5 * x * (1.0 + lax.erf(x * np.float32(math.sqrt(0.5))))


def _mod_kernel(c_ref, w_ref, b_ref, o_ref):
    c = c_ref[...]
    ca = c * jax.nn.sigmoid(c)
    o_ref[0] = jnp.dot(ca, w_ref[0], preferred_element_type=F32) + b_ref[0]


def _modulation(c, w, b):
    n_layers, d, d3 = w.shape
    bsz = c.shape[0]
    n_col = d3 // d
    return pl.pallas_call(
        _mod_kernel,
        out_shape=jax.ShapeDtypeStruct((n_layers, bsz, d3), F32),
        grid=(n_layers, n_col),
        in_specs=[
            pl.BlockSpec((bsz, d), lambda i, j: (0, 0)),
            pl.BlockSpec((1, d, d), lambda i, j: (i, 0, j)),
            pl.BlockSpec((1, 1, d), lambda i, j: (i, 0, j)),
        ],
        out_specs=pl.BlockSpec((1, bsz, d), lambda i, j: (i, 0, j)),
        compiler_params=pltpu.CompilerParams(
            dimension_semantics=("parallel", "parallel"),
            vmem_limit_bytes=_vmem_limit(d * d * 4 + 2 * bsz * d * 4, 0, 0)),
        name="adaln_modulation",
    )(c, w, b.reshape(n_layers, 1, d3))


def _mixer_kernel(x_ref, xp_ref, xn_ref, mod_ref, ng_ref, win_ref, cw_ref, cb_ref,
                  clg_ref, clb_ref, slg_ref, slb_ref, sgw_ref, sgb_ref, wout_ref,
                  o_ref, abuf_ref, cbuf_ref, cat_ref, *, tile, d, conv_ch, sg_ch):
    t = pl.program_id(1)
    n_t = pl.num_programs(1)
    mod = mod_ref[0]
    shift, scale, gate = mod[:, :d], mod[:, d:2 * d], mod[:, 2 * d:]
    g = ng_ref[...]
    n_slab = conv_ch // V7X_LANES
    pad = CONV_WIDTH // 2

    x = x_ref[0]
    h = _rms_modulate(x, g, shift, scale).astype(BF16)
    xh = jnp.concatenate([xp_ref[0], xn_ref[0]], axis=0)
    hh = _rms_modulate(xh, g, shift, scale).astype(BF16)

    za = jnp.dot(h, win_ref[:, :2 * conv_ch], preferred_element_type=F32)
    a_main = za[:, :conv_ch] * jax.nn.sigmoid(za[:, conv_ch:])
    zh = jnp.dot(hh, win_ref[:, :2 * conv_ch], preferred_element_type=F32)
    a_halo = zh[:, :conv_ch] * jax.nn.sigmoid(zh[:, conv_ch:])
    a_prev = jnp.where(t > 0, a_halo[:HALO_ROWS], 0.0)
    a_next = jnp.where(t < n_t - 1, a_halo[HALO_ROWS:], 0.0)
    for j in range(n_slab):
        ls = slice(j * V7X_LANES, (j + 1) * V7X_LANES)
        abuf_ref[j, 0:HALO_ROWS, :] = a_prev[:, ls]
        abuf_ref[j, HALO_ROWS:HALO_ROWS + tile, :] = a_main[:, ls]
        abuf_ref[j, HALO_ROWS + tile:, :] = a_next[:, ls]

    rows = 64
    base = HALO_ROWS - pad
    for j in range(n_slab):
        ls = slice(j * V7X_LANES, (j + 1) * V7X_LANES)
        for r0 in range(0, tile, rows):
            acc = jnp.zeros((rows, V7X_LANES), F32) + cb_ref[:, ls]
            for k in range(CONV_WIDTH):
                acc = acc + cw_ref[k:k + 1, ls] * abuf_ref[j, r0 + base + k:r0 + base + k + rows, :]
            cbuf_ref[r0:r0 + rows, ls] = acc
    a = _layernorm(cbuf_ref[...], clg_ref[...], clb_ref[...])
    a = a * jax.nn.sigmoid(a)
    cat_ref[:, :conv_ch] = a.astype(BF16)

    zu = jnp.dot(h, win_ref[:, 2 * conv_ch:2 * conv_ch + sg_ch], preferred_element_type=F32)
    zv = jnp.dot(h, win_ref[:, 2 * conv_ch + sg_ch:], preferred_element_type=F32)
    u = _gelu_exact(zu)
    v = _layernorm(_gelu_exact(zv), slg_ref[...], slb_ref[...]).astype(BF16)
    hd = sg_ch // SG_HEADS
    for n in range(tile // SG_CHUNK):
        rs = slice(n * SG_CHUNK, (n + 1) * SG_CHUNK)
        for gi in range(SG_HEADS):
            cs = slice(gi * hd, (gi + 1) * hd)
            vp = jnp.dot(sgw_ref[gi], v[rs, cs], preferred_element_type=F32) + sgb_ref[gi]
            cat_ref[rs, conv_ch + gi * hd:conv_ch + (gi + 1) * hd] = (u[rs, cs] * vp).astype(BF16)

    y = jnp.dot(cat_ref[...], wout_ref[...], preferred_element_type=F32)
    o_ref[0] = x + gate * y


def _mixer_layer(x, mod, norm_g, w_in, conv_w, conv_b, cln_g, cln_b, sln_g, sln_b,
                 sg_w, sg_b, w_out, *, tile=512):
    bsz, seq, d = x.shape
    conv_ch = conv_w.shape[1]
    sg_ch = sln_g.shape[0]
    in_ab = w_in.shape[1]
    mix = w_out.shape[0]
    assert in_ab == 2 * conv_ch + 2 * sg_ch and mix == conv_ch + sg_ch
    assert seq % tile == 0 and tile % SG_CHUNK == 0 and tile % HALO_ROWS == 0
    assert HALO_ROWS >= CONV_WIDTH // 2 and conv_ch % V7X_LANES == 0
    assert sg_ch // SG_HEADS == V7X_LANES
    n_t = seq // tile
    hb = tile // HALO_ROWS
    n_halo_blocks = seq // HALO_ROWS
    row = lambda a: a.reshape(1, -1)
    kern = functools.partial(_mixer_kernel, tile=tile, d=d, conv_ch=conv_ch, sg_ch=sg_ch)
    const2 = lambda b, t: (0, 0)
    const3 = lambda b, t: (0, 0, 0)
    pipelined = 2 * tile * d * 4 + 2 * HALO_ROWS * d * 4 + 3 * d * 4
    resident = (d * in_ab + mix * d) * 2 + SG_HEADS * SG_CHUNK * SG_CHUNK * 2 + (1 << 20)
    scratch = (conv_ch // V7X_LANES) * (tile + 2 * HALO_ROWS) * V7X_LANES * 4 + tile * conv_ch * 4 + tile * mix * 2
    temps = 6 * tile * in_ab * 4
    return pl.pallas_call(
        kern,
        out_shape=jax.ShapeDtypeStruct((bsz, seq, d), F32),
        grid=(bsz, n_t),
        in_specs=[
            pl.BlockSpec((1, tile, d), lambda b, t: (b, t, 0)),
            pl.BlockSpec((1, HALO_ROWS, d), lambda b, t: (b, jnp.maximum(t * hb - 1, 0), 0)),
            pl.BlockSpec((1, HALO_ROWS, d),
                         lambda b, t: (b, jnp.minimum((t + 1) * hb, n_halo_blocks - 1), 0)),
            pl.BlockSpec((1, 1, 3 * d), lambda b, t: (b, 0, 0)),
            _resident((1, d), const2),
            _resident((d, in_ab), const2),
            _resident((CONV_WIDTH, conv_ch), const2),
            _resident((1, conv_ch), const2),
            _resident((1, conv_ch), const2),
            _resident((1, conv_ch), const2),
            _resident((1, sg_ch), const2),
            _resident((1, sg_ch), const2),
            _resident((SG_HEADS, SG_CHUNK, SG_CHUNK), const3),
            _resident((SG_HEADS, SG_CHUNK, 1), const3),
            _resident((mix, d), const2),
        ],
        out_specs=pl.BlockSpec((1, tile, d), lambda b, t: (b, t, 0)),
        scratch_shapes=[
            pltpu.VMEM((conv_ch // V7X_LANES, tile + 2 * HALO_ROWS, V7X_LANES), F32),
            pltpu.VMEM((tile, conv_ch), F32),
            pltpu.VMEM((tile, mix), BF16),
        ],
        compiler_params=pltpu.CompilerParams(
            dimension_semantics=("parallel", "parallel"),
            vmem_limit_bytes=_vmem_limit(pipelined, resident + scratch, temps)),
        name="conv_gmlp_mixer",
    )(x, x, x, mod.reshape(bsz, 1, 3 * d), row(norm_g), w_in.astype(BF16), conv_w, row(conv_b),
      row(cln_g), row(cln_b), row(sln_g), row(sln_b), sg_w.astype(BF16),
      sg_b.reshape(SG_HEADS, SG_CHUNK, 1), w_out.astype(BF16))


def _mlp_kernel(*refs, d, d_ff, ff_chunk, pre_proj, final_norm):
    it = iter(refs)
    x_ref = next(it)
    if pre_proj:
        attn_ref, pmod_ref, wo_ref = next(it), next(it), next(it)
    mod_ref, ng_ref, w1_ref, w2_ref = next(it), next(it), next(it), next(it)
    fg_ref = next(it) if final_norm else None
    o_ref = next(it)

    x = x_ref[0]
    if pre_proj:
        pgate = pmod_ref[0][:, 2 * d:]
        x = x + pgate * jnp.dot(attn_ref[0], wo_ref[...], preferred_element_type=F32)
    mod = mod_ref[0]
    shift, scale, gate = mod[:, :d], mod[:, d:2 * d], mod[:, 2 * d:]
    h = _rms_modulate(x, ng_ref[...], shift, scale).astype(BF16)
    acc = jnp.zeros(x.shape, F32)
    for c0 in range(0, d_ff, ff_chunk):
        a = jnp.dot(h, w1_ref[:, c0:c0 + ff_chunk], preferred_element_type=F32)
        a = jnp.square(jnp.maximum(a, 0.0)).astype(BF16)
        acc = acc + jnp.dot(a, w2_ref[c0:c0 + ff_chunk, :], preferred_element_type=F32)
    out = x + gate * acc
    if final_norm:
        ms = jnp.mean(out * out, axis=-1, keepdims=True)
        out = out * lax.rsqrt(ms + RMS_EPS) * fg_ref[...]
    o_ref[0] = out


def _mlp_layer(x, mod, norm_g, w1, w2, *, attn=None, attn_mod=None, attn_w_out=None,
               final_g=None, tile=512, ff_chunk=1024):
    bsz, seq, d = x.shape
    d_ff = w1.shape[1]
    assert seq % tile == 0 and d_ff % ff_chunk == 0
    pre_proj = attn is not None
    final_norm = final_g is not None
    const2 = lambda b, t: (0, 0)
    tile_spec = pl.BlockSpec((1, tile, d), lambda b, t: (b, t, 0))
    mod_spec = pl.BlockSpec((1, 1, 3 * d), lambda b, t: (b, 0, 0))
    in_specs, args = [tile_spec], [x]
    pipelined = 2 * tile * d * 4 + 3 * d * 4
    resident = 2 * d * d_ff * 2 + 2 * d * 4
    if pre_proj:
        in_specs += [tile_spec, mod_spec, _resident((d, d), const2)]
        args += [attn, attn_mod.reshape(bsz, 1, 3 * d), attn_w_out.astype(BF16)]
        pipelined += tile * d * 2 + 3 * d * 4
        resident += d * d * 2
    in_specs += [mod_spec, _resident((1, d), const2), _resident((d, d_ff), const2),
                 _resident((d_ff, d), const2)]
    args += [mod.reshape(bsz, 1, 3 * d), norm_g.reshape(1, d), w1.astype(BF16), w2.astype(BF16)]
    if final_norm:
        in_specs.append(_resident((1, d), const2))
        args.append(final_g.reshape(1, d))
    kern = functools.partial(_mlp_kernel, d=d, d_ff=d_ff, ff_chunk=ff_chunk,
                             pre_proj=pre_proj, final_norm=final_norm)
    temps = 4 * tile * d * 4 + 2 * tile * ff_chunk * 4
    return pl.pallas_call(
        kern,
        out_shape=jax.ShapeDtypeStruct((bsz, seq, d), F32),
        grid=(bsz, seq // tile),
        in_specs=in_specs,
        out_specs=tile_spec,
        compiler_params=pltpu.CompilerParams(
            dimension_semantics=("parallel", "parallel"),
            vmem_limit_bytes=_vmem_limit(pipelined, resident, temps)),
        name="relu2_mlp",
    )(*args)


def _qkv_kernel(x_ref, mod_ref, ng_ref, w_ref, q_ref, k1_ref, k2_ref, v_ref, kmax_ref, *, d):
    mod = mod_ref[0]
    shift, scale = mod[:, :d], mod[:, d:2 * d]
    h = _rms_modulate(x_ref[0], ng_ref[...], shift, scale).astype(BF16)
    q = jnp.dot(h, w_ref[:, :d], preferred_element_type=F32)
    q_ref[0] = (q * np.float32(DIFF_HEAD_DIM ** -0.5 * LOG2E)).astype(BF16)
    k = jnp.dot(h, w_ref[:, d:2 * d], preferred_element_type=F32)
    first = (lax.broadcasted_iota(jnp.int32, k.shape, 1) % (2 * DIFF_HEAD_DIM)) < DIFF_HEAD_DIM
    k1_ref[0] = jnp.where(first, k, 0.0).astype(BF16)
    k2_ref[0] = jnp.where(first, 0.0, k).astype(BF16)
    kmax_ref[0, 0] = jnp.max(jnp.abs(k), axis=0, keepdims=True)
    v_ref[0] = jnp.dot(h, w_ref[:, 2 * d:], preferred_element_type=F32).astype(BF16)


def _qkv_projection(x, mod, norm_g, w_qkv, *, tile=512):
    bsz, seq, d = x.shape
    assert seq % tile == 0 and w_qkv.shape == (d, 3 * d)
    n_t = seq // tile
    tile_spec = pl.BlockSpec((1, tile, d), lambda b, t: (b, t, 0))
    out = jax.ShapeDtypeStruct((bsz, seq, d), BF16)
    const2 = lambda b, t: (0, 0)
    q, k1, k2, v, kmax = pl.pallas_call(
        functools.partial(_qkv_kernel, d=d),
        out_shape=(out, out, out, out, jax.ShapeDtypeStruct((bsz, n_t, 1, d), F32)),
        grid=(bsz, n_t),
        in_specs=[tile_spec, pl.BlockSpec((1, 1, 3 * d), lambda b, t: (b, 0, 0)),
                  _resident((1, d), const2), _resident((d, 3 * d), const2)],
        out_specs=(tile_spec, tile_spec, tile_spec, tile_spec,
                   pl.BlockSpec((1, 1, 1, d), lambda b, t: (b, t, 0, 0))),
        compiler_params=pltpu.CompilerParams(
            dimension_semantics=("parallel", "parallel"),
            vmem_limit_bytes=_vmem_limit(tile * d * 4 + 4 * tile * d * 2 + 4 * d * 4,
                                         3 * d * d * 2, 6 * tile * d * 4)),
        name="qkv_projection",
    )(x, mod.reshape(bsz, 1, 3 * d), norm_g.reshape(1, d), w_qkv.astype(BF16))
    return q, k1, k2, v, kmax.reshape(bsz, n_t, d)


def _t5_bucket_table(max_dist):
    nb = N_BUCKETS // 2
    max_exact = nb // 2
    assert (N_BUCKETS, MAX_DISTANCE) == (32, 128)
    thresholds = np.array([8, 12, 16, 23, 32, 46, 64, 91])
    rel = np.arange(-max_dist, max_dist + 1)
    n = np.abs(rel)
    large = max_exact + (n[:, None] >= thresholds[None, :]).sum(axis=1) - 1
    large = np.minimum(large, nb - 1)
    return (np.where(rel > 0, nb, 0) + np.where(n < max_exact, n, large)).astype(np.int32)


FAR_DISTANCE = 91


def _band_kernel(tab_ref, idx_ref, o_ref):
    hd = pl.program_id(0)
    idx = idx_ref[...]
    acc = jnp.zeros(idx.shape, F32)
    for b in range(N_BUCKETS):
        acc = jnp.where(idx == b, tab_ref[b, hd], acc)
    o_ref[0] = acc * np.float32(LOG2E)


N_BAND_TILES = 5


def _bias_band(rel_bias_table, tile):
    n_heads = rel_bias_table.shape[1]
    assert tile + 1 >= FAR_DISTANCE
    half = N_BAND_TILES // 2
    buckets = _t5_bucket_table((half + 1) * tile)
    t = np.arange(N_BAND_TILES)[:, None, None]
    i = np.arange(tile)[None, :, None]
    j = np.arange(tile)[None, None, :]
    idx = buckets[(t - half) * tile + j - i + (half + 1) * tile]
    assert (idx[0] == idx[0, 0, 0]).all() and (idx[-1] == idx[-1, 0, 0]).all()
    blk = (N_BAND_TILES, tile, tile)
    return pl.pallas_call(
        _band_kernel,
        out_shape=jax.ShapeDtypeStruct((n_heads,) + blk, F32),
        grid=(n_heads,),
        in_specs=[pl.BlockSpec(memory_space=pltpu.SMEM),
                  pl.BlockSpec(blk, lambda h: (0, 0, 0))],
        out_specs=pl.BlockSpec((1,) + blk, lambda h: (h, 0, 0, 0)),
        compiler_params=pltpu.CompilerParams(
            dimension_semantics=("parallel",),
            vmem_limit_bytes=_vmem_limit(2 * math.prod(blk) * 4, 0, 4 * math.prod(blk) * 4)),
        name="t5_bias_band",
    )(rel_bias_table, jnp.asarray(idx))


_NT = (((1,), (1,)), ((), ()))
L_MIN, L_MAX = 1e-30, 1e30


def _attn_kernel(bmax_ref, q_ref, k1_ref, k2_ref, v_ref, kmax_ref, band_ref, lamv_ref, sg_ref, o_ref,
                 m_ref, l_ref, acc_ref, lw_ref, *, tile, n_chunks, lambda_init, unroll):
    hd = pl.program_id(1)
    qi = pl.program_id(2)
    q = q_ref[0]
    k_refs = (k1_ref, k2_ref)
    n_blk = tile // V7X_LANES
    b_max = bmax_ref[hd]

    def chunk_operands(kc):
        start = pl.multiple_of(kc * tile, tile)
        return [k_ref[0, pl.ds(start, tile), :] for k_ref in k_refs], v_ref[0, pl.ds(start, tile), :]

    def band_tile_index(kc):
        return jnp.clip(kc - qi + N_BAND_TILES // 2, 0, N_BAND_TILES - 1)

    kabs = jnp.max(kmax_ref[0], axis=0, keepdims=True) * np.float32(1.0 + 2.0 ** -6)
    lane = lax.broadcasted_iota(jnp.int32, kabs.shape, 1)
    qa = jnp.abs(q)
    offsets = []
    for j in range(2):
        mine = (lane < DIFF_HEAD_DIM) if j == 0 else (lane >= DIFF_HEAD_DIM)
        kb = jnp.broadcast_to(jnp.where(mine, kabs, 0.0), (V7X_LANES, V7X_LANES)).astype(BF16)
        offsets.append(lax.dot_general(qa, kb, _NT, preferred_element_type=F32) + b_max)
    lw_ref[...] = jnp.zeros(lw_ref.shape, F32)
    acc_ref[...] = jnp.zeros(acc_ref.shape, F32)

    def fast_update(kc, carry):
        ks, v = chunk_operands(kc)
        band_tile = band_tile_index(kc)
        for j in range(2):
            s = lax.dot_general(q, ks[j], _NT, preferred_element_type=F32)
            lsum = None
            ps = []
            for jb in range(n_blk):
                cols = slice(jb * V7X_LANES, (jb + 1) * V7X_LANES)
                pj = jnp.exp2(s[:, cols] + band_ref[0, band_tile, :, cols] - offsets[j])
                lsum = pj if lsum is None else lsum + pj
                ps.append(pj.astype(BF16))
            lw_ref[j] += lsum
            acc_ref[j] += jnp.dot(jnp.concatenate(ps, axis=1), v, preferred_element_type=F32)
        return carry

    lax.fori_loop(0, n_chunks, fast_update, 0, unroll=unroll)

    l_fast = jnp.sum(lw_ref[...], axis=-1, keepdims=True)
    l_ref[...] = l_fast
    redo = jnp.logical_not(jnp.logical_and(jnp.min(l_fast) >= L_MIN, jnp.max(l_fast) <= L_MAX))

    @pl.when(redo)
    def _():
        m_ref[...] = jnp.full(m_ref.shape, -jnp.inf, F32)
        l_ref[...] = jnp.zeros(l_ref.shape, F32)
        acc_ref[...] = jnp.zeros(acc_ref.shape, F32)

        def update(kc, carry):
            ks, v = chunk_operands(kc)
            band_tile = band_tile_index(kc)
            for j in range(2):
                s = lax.dot_general(q, ks[j], _NT, preferred_element_type=F32) + band_ref[0, band_tile]
                m_prev = m_ref[j]
                m_new = jnp.maximum(m_prev, jnp.max(s, axis=-1, keepdims=True))
                alpha = jnp.exp2(m_prev - m_new)
                p = jnp.exp2(s - m_new)
                l_ref[j] = alpha * l_ref[j] + jnp.sum(p, axis=-1, keepdims=True)
                acc_ref[j] = alpha * acc_ref[j] + jnp.dot(p.astype(BF16), v, preferred_element_type=F32)
                m_ref[j] = m_new
            return carry

        lax.fori_loop(0, n_chunks, update, 0)

    lv = lamv_ref[...]
    lam = (jnp.exp(jnp.sum(lv[0:1] * lv[1:2], axis=-1, keepdims=True))
           - jnp.exp(jnp.sum(lv[2:3] * lv[3:4], axis=-1, keepdims=True)) + lambda_init)
    o = acc_ref[0] / l_ref[0] - lam * (acc_ref[1] / l_ref[1])
    ms = jnp.mean(o * o, axis=-1, keepdims=True)
    o = o * lax.rsqrt(ms + SUBLN_EPS) * sg_ref[...] * (1.0 - lambda_init)
    o_ref[0] = o.astype(BF16)


def _diff_attention(q, k1, k2, v, kmax, rel_bias_table, lamv, subln_g, lambda_init, *, tile=512,
                    unroll=16):
    bsz, seq, d = q.shape
    hw = 2 * DIFF_HEAD_DIM
    n_heads = d // hw
    assert hw == V7X_LANES and seq % tile == 0 and tile >= FAR_DISTANCE
    n_chunks = seq // tile
    n_kmax = kmax.shape[1]
    buckets = _t5_bucket_table(seq - 1)
    assert (buckets[:seq - FAR_DISTANCE] == buckets[0]).all()
    assert (buckets[seq - 1 + FAR_DISTANCE:] == buckets[-1]).all()
    bmax = jnp.max(rel_bias_table, axis=0) * np.float32(LOG2E)
    band = _bias_band(rel_bias_table, tile)
    q_spec = pl.BlockSpec((1, tile, hw), lambda b, h, t: (b, t, h))
    kv_spec = pl.BlockSpec((1, seq, hw), lambda b, h, t: (b, 0, h))
    kern = functools.partial(_attn_kernel, tile=tile, n_chunks=n_chunks, lambda_init=lambda_init,
                             unroll=unroll)
    pipelined = (2 * tile * hw * 2 + 3 * seq * hw * 2 + N_BAND_TILES * tile * tile * 4
                 + n_kmax * hw * 4)
    scratch = 2 * tile * V7X_LANES * 4 * 4
    temps = 8 * tile * tile * 4
    return pl.pallas_call(
        kern,
        out_shape=jax.ShapeDtypeStruct((bsz, seq, d), BF16),
        grid=(bsz, n_heads, n_chunks),
        in_specs=[
            pl.BlockSpec(memory_space=pltpu.SMEM),
            q_spec, kv_spec, kv_spec, kv_spec,
            pl.BlockSpec((1, n_kmax, hw), lambda b, h, t: (b, 0, h)),
            pl.BlockSpec((1, N_BAND_TILES, tile, tile), lambda b, h, t: (h, 0, 0, 0)),
            pl.BlockSpec((4, DIFF_HEAD_DIM), lambda b, h, t: (0, 0)),
            pl.BlockSpec((1, hw), lambda b, h, t: (0, 0)),
        ],
        out_specs=q_spec,
        scratch_shapes=[
            pltpu.VMEM((2, tile, 1), F32),
            pltpu.VMEM((2, tile, 1), F32),
            pltpu.VMEM((2, tile, hw), F32),
            pltpu.VMEM((2, tile, hw), F32),
        ],
        compiler_params=pltpu.CompilerParams(
            dimension_semantics=("parallel", "parallel", "parallel"),
            vmem_limit_bytes=_vmem_limit(pipelined, scratch, temps)),
        name="diff_flash_attention",
    )(bmax, q, k1, k2, v, kmax, band, lamv, subln_g.reshape(1, hw))


def kernel(x, c, mix_norm_g, mix_mod_w, mix_mod_b, ab_w_in, conv_w, conv_b, conv_ln_g, conv_ln_b, sg_ln_g, sg_ln_b, sg_w, sg_b, ab_w_out, attn_w_qkv, lam_q1, lam_k1, lam_q2, lam_k2, subln_g, attn_w_out, rel_bias_table, mlp_norm_g, mlp_mod_w, mlp_mod_b, mlp_w1, mlp_w2, final_norm_g):
    depth = mix_norm_g.shape[0]
    mix_mod = _modulation(c, mix_mod_w, mix_mod_b)
    mlp_mod = _modulation(c, mlp_mod_w, mlp_mod_b)
    for i in range(depth):
        j = i // 2
        last = final_norm_g if i == depth - 1 else None
        if i % 2 == 0:
            x = _mixer_layer(x, mix_mod[i], mix_norm_g[i], ab_w_in[j], conv_w[j], conv_b[j],
                             conv_ln_g[j], conv_ln_b[j], sg_ln_g[j], sg_ln_b[j], sg_w[j], sg_b[j],
                             ab_w_out[j])
            x = _mlp_layer(x, mlp_mod[i], mlp_norm_g[i], mlp_w1[i], mlp_w2[i], final_g=last)
        else:
            lambda_init = 0.8 - 0.6 * math.exp(-0.3 * i)
            q, k1, k2, v, kmax = _qkv_projection(x, mix_mod[i], mix_norm_g[i], attn_w_qkv[j])
            lamv = jnp.stack([lam_q1[j], lam_k1[j], lam_q2[j], lam_k2[j]])
            o = _diff_attention(q, k1, k2, v, kmax, rel_bias_table, lamv, subln_g[j], lambda_init)
            x = _mlp_layer(x, mlp_mod[i], mlp_norm_g[i], mlp_w1[i], mlp_w2[i], attn=o,
                           attn_mod=mix_mod[i], attn_w_out=attn_w_out[j], final_g=last)
    return x
```
